```python
import math
import jax, jax.numpy as jnp
from jax import lax
import numpy as np

D_MODEL = 2048
BATCH = 2
SEQ = 4096
DEPTH = 1

MEM_LEN = 256
MIX_A_WIDTH = D_MODEL // 2
MIX_B_WIDTH = D_MODEL - MIX_A_WIDTH
A_GROUPS = 8
A_HEAD = MIX_A_WIDTH // A_GROUPS
A_CHUNK = 128
B_HEAD_K = 128
B_HEAD_V = 128
B_HEADS = MIX_B_WIDTH // B_HEAD_V
B_CHUNK = 64
XA_HEADS = 4
XA_HEAD = D_MODEL // XA_HEADS
PEER_HEADS = 8
PEER_NKEYS = 128
PEER_EXPERTS = PEER_NKEYS * PEER_NKEYS
PEER_QDIM = 256
PEER_HALF = PEER_QDIM // 2
PEER_TOPK = 16
PEER_TOKEN_BLOCK = 128
LN_EPS = 1e-5
DN_ALPHA = (2.0 * DEPTH) ** 0.25
DN_BETA = (8.0 * DEPTH) ** -0.25
IN_COLS = 2 * MIX_A_WIDTH + 4 * MIX_B_WIDTH
SPLITS = [MIX_A_WIDTH, 2 * MIX_A_WIDTH, 2 * MIX_A_WIDTH + MIX_B_WIDTH,
          2 * MIX_A_WIDTH + 2 * MIX_B_WIDTH, 2 * MIX_A_WIDTH + 3 * MIX_B_WIDTH]

kernel_name = 'hybrid_sgu_hgrn2_peer_deepnorm'


def layer_norm(x, g, b):
    xf = x.astype(jnp.float32)
    mu = jnp.mean(xf, axis=-1, keepdims=True)
    var = jnp.mean(jnp.square(xf - mu), axis=-1, keepdims=True)
    return ((xf - mu) * lax.rsqrt(var + LN_EPS) * g + b).astype(x.dtype)


def rms_norm(x, g):
    xf = x.astype(jnp.float32)
    return xf * lax.rsqrt(jnp.mean(jnp.square(xf), axis=-1, keepdims=True) + LN_EPS) * g


def chunked_sgu(u, v, w_s, b_s, vg, vb):
    B_, S_, _ = u.shape
    nc = S_ // A_CHUNK
    u = jax.nn.gelu(u)
    v = jax.nn.gelu(v).reshape(B_, S_, A_GROUPS, A_HEAD)
    v = layer_norm(v, vg, vb).reshape(B_, nc, A_CHUNK, A_GROUPS, A_HEAD)
    causal = jnp.tril(jnp.ones((A_CHUNK, A_CHUNK), dtype=bool))
    w = jnp.where(causal, w_s, 0.0)
    z = jnp.einsum('gts,bcsgd->bctgd', w, v) + b_s.T[None, None, :, :, None]
    return u * z.reshape(B_, S_, MIX_A_WIDTH)


def hgrn2(q, f_logit, i, g, lb, gn):
    B_, S_, _ = q.shape
    nc = S_ // B_CHUNK
    f32 = jnp.float32
    lbf = lb.astype(f32)
    f = lbf + (1.0 - lbf) * jax.nn.sigmoid(f_logit.astype(f32))
    log_f = jnp.log(f)
    k = 1.0 - f

    def to_chunks(t, d):
        return t.reshape(B_, nc, B_CHUNK, B_HEADS, d).transpose(1, 0, 3, 2, 4)

    qc = to_chunks(q.astype(f32), B_HEAD_K)
    kc = to_chunks(k, B_HEAD_K)
    lc = to_chunks(log_f, B_HEAD_K)
    ic = to_chunks(i.astype(f32), B_HEAD_V)
    causal = jnp.tril(jnp.ones((B_CHUNK, B_CHUNK), dtype=bool))[:, :, None]

    def step(state, xs):
        qb, kb, lfb, ib = xs
        a = jnp.cumsum(lfb, axis=2)
        a_last = a[:, :, -1, :]
        diff = a[:, :, :, None, :] - a[:, :, None, :, :]
        decay = jnp.exp(jnp.where(causal, diff, -jnp.inf))
        scores = jnp.einsum('bhtk,bhsk,bhtsk->bhts', qb, kb, decay)
        o = (jnp.einsum('bhts,bhsv->bhtv', scores, ib)
             + jnp.einsum('bhtk,bhkv->bhtv', qb * jnp.exp(a), state))
        new_state = (jnp.exp(a_last)[..., None] * state
                     + jnp.einsum('bhsk,bhsv->bhkv', kb * jnp.exp(a_last[:, :, None, :] - a), ib))
        return new_state, o

    s0 = jnp.zeros((B_, B_HEADS, B_HEAD_K, B_HEAD_V), f32)
    _, oc = lax.scan(step, s0, (qc, kc, lc, ic))
    o = oc.transpose(1, 0, 3, 2, 4).reshape(B_, S_, B_HEADS, B_HEAD_V)
    o = rms_norm(o, gn).reshape(B_, S_, MIX_B_WIDTH) * jax.nn.silu(g.astype(f32))
    return o.astype(q.dtype)


def memory_cross_attention(x, mem, wq, wk, wv, wo):
    B_, S_, _ = x.shape
    M_ = mem.shape[1]
    q = (x @ wq).reshape(B_, S_, XA_HEADS, XA_HEAD)
    k = (mem @ wk).reshape(B_, M_, XA_HEADS, XA_HEAD)
    v = (mem @ wv).reshape(B_, M_, XA_HEADS, XA_HEAD)
    s = jnp.einsum('bshd,bmhd->bhsm', q, k).astype(jnp.float32) * (1.0 / math.sqrt(XA_HEAD))
    p = jax.nn.softmax(s, axis=-1).astype(v.dtype)
    o = jnp.einsum('bhsm,bmhd->bshd', p, v).reshape(B_, S_, D_MODEL)
    return o @ wo


def peer(x, wq, sub_k1, sub_k2, u_tab, v_tab):
    B_, S_, _ = x.shape
    f32 = jnp.float32
    q = (x @ wq).astype(f32).reshape(B_, S_, PEER_HEADS, 2, PEER_HALF)
    s1 = jnp.einsum('bshd,nd->bshn', q[..., 0, :], sub_k1.astype(f32))
    s2 = jnp.einsum('bshd,nd->bshn', q[..., 1, :], sub_k2.astype(f32))
    v1, i1 = lax.top_k(s1, PEER_TOPK)
    v2, i2 = lax.top_k(s2, PEER_TOPK)
    cand = (v1[..., :, None] + v2[..., None, :]).reshape(B_, S_, PEER_HEADS, PEER_TOPK * PEER_TOPK)
    cand_idx = (i1[..., :, None] * PEER_NKEYS + i2[..., None, :]).reshape(B_, S_, PEER_HEADS, PEER_TOPK * PEER_TOPK)
    top_s, top_pos = lax.top_k(cand, PEER_TOPK)
    experts = jnp.take_along_axis(cand_idx, top_pos, axis=-1)
    gates = jax.nn.softmax(top_s, axis=-1).astype(x.dtype)
    nb = (B_ * S_) // PEER_TOKEN_BLOCK
    xb = x.reshape(nb, PEER_TOKEN_BLOCK, D_MODEL)
    eb = experts.reshape(nb, PEER_TOKEN_BLOCK, PEER_HEADS, PEER_TOPK)
    gb = gates.reshape(nb, PEER_TOKEN_BLOCK, PEER_HEADS, PEER_TOPK)

    def block(args):
        xt, et, gt = args
        u = u_tab[et]
        h = jax.nn.gelu(jnp.einsum('td,thkd->thk', xt, u))
        return jnp.einsum('thk,thkd->td', gt * h, v_tab[et])

    out = lax.map(block, (xb, eb, gb))
    return out.reshape(B_, S_, D_MODEL)


def setup_inputs(seed: int = 0) -> dict:
    key = jax.random.key(seed)
    ks = jax.random.split(key, 32)
    n = jax.random.normal
    L = DEPTH
    D = D_MODEL
    return {
        'x': n(ks[0], (BATCH, SEQ, D), jnp.float32),
        'mem': n(ks[1], (BATCH, MEM_LEN, D), jnp.float32),
        'w_in': n(ks[2], (L, D, IN_COLS), jnp.float32) * D ** -0.5,
        'sgu_w': n(ks[3], (L, A_GROUPS, A_CHUNK, A_CHUNK), jnp.float32) * A_CHUNK ** -0.5,
        'sgu_b': 1.0 + 0.02 * n(ks[4], (L, A_GROUPS, A_CHUNK), jnp.float32),
        'sgu_ln_g': 1.0 + 0.02 * n(ks[5], (L, A_GROUPS, A_HEAD), jnp.float32),
        'sgu_ln_b': 0.02 * n(ks[6], (L, A_GROUPS, A_HEAD), jnp.float32),
        'hgrn_lb_logits': 0.5 * n(ks[7], (L + 1, B_HEADS * B_HEAD_K), jnp.float32),
        'hgrn_norm_g': 1.0 + 0.02 * n(ks[8], (L, B_HEADS, B_HEAD_V), jnp.float32),
        'w_out': n(ks[9], (L, D, D), jnp.float32) * D ** -0.5 * DN_BETA,
        'ln1_g': 1.0 + 0.02 * n(ks[10], (L, D), jnp.float32),
        'ln1_b': 0.02 * n(ks[11], (L, D), jnp.float32),
        'xa_wq': n(ks[12], (L, D, D), jnp.float32) * D ** -0.5,
        'xa_wk': n(ks[13], (L, D, D), jnp.float32) * D ** -0.5,
        'xa_wv': n(ks[14], (L, D, D), jnp.float32) * D ** -0.5 * DN_BETA,
        'xa_wo': n(ks[15], (L, D, D), jnp.float32) * D ** -0.5 * DN_BETA,
        'ln2_g': 1.0 + 0.02 * n(ks[16], (L, D), jnp.float32),
        'ln2_b': 0.02 * n(ks[17], (L, D), jnp.float32),
        'peer_wq': n(ks[18], (L, D, PEER_HEADS * PEER_QDIM), jnp.float32) * D ** -0.5,
        'peer_k1': n(ks[19], (L, PEER_NKEYS, PEER_HALF), jnp.float32) * PEER_HALF ** -0.5,
        'peer_k2': n(ks[20], (L, PEER_NKEYS, PEER_HALF), jnp.float32) * PEER_HALF ** -0.5,
        'peer_u': n(ks[21], (L, PEER_EXPERTS, D), jnp.float32) * D ** -0.5,
        'peer_v': n(ks[22], (L, PEER_EXPERTS, D), jnp.float32) * DN_BETA * PEER_HEADS ** -0.5,
        'ln3_g': 1.0 + 0.02 * n(ks[23], (L, D), jnp.float32),
        'ln3_b': 0.02 * n(ks[24], (L, D), jnp.float32),
    }


def reference(x, mem, w_in, sgu_w, sgu_b, sgu_ln_g, sgu_ln_b, hgrn_lb_logits, hgrn_norm_g,
              w_out, ln1_g, ln1_b, xa_wq, xa_wk, xa_wv, xa_wo, ln2_g, ln2_b,
              peer_wq, peer_k1, peer_k2, peer_u, peer_v, ln3_g, ln3_b):
    lb_all = jnp.cumsum(jax.nn.softmax(hgrn_lb_logits.astype(jnp.float32), axis=0), axis=0)
    for l in range(DEPTH):
        h = x @ w_in[l]
        ua, va, qb, fb, ib, gb = jnp.split(h, SPLITS, axis=-1)
        ya = chunked_sgu(ua, va, sgu_w[l], sgu_b[l], sgu_ln_g[l], sgu_ln_b[l])
        yb = hgrn2(qb, fb, ib, gb, lb_all[l], hgrn_norm_g[l])
        mix = jnp.concatenate([ya, yb.astype(ya.dtype)], axis=-1) @ w_out[l]
        x = layer_norm(DN_ALPHA * x + mix, ln1_g[l], ln1_b[l])
        xa = memory_cross_attention(x, mem, xa_wq[l], xa_wk[l], xa_wv[l], xa_wo[l])
        x = layer_norm(DN_ALPHA * x + xa, ln2_g[l], ln2_b[l])
        ff = peer(x, peer_wq[l], peer_k1[l], peer_k2[l], peer_u[l], peer_v[l])
        x = layer_norm(DN_ALPHA * x + ff, ln3_g[l], ln3_b[l])
    return x
```

```python
import functools
import math

import jax
import jax.numpy as jnp
from jax import lax
from jax.experimental import pallas as pl
from jax.experimental.pallas import tpu as pltpu

F32 = jnp.float32
BF16 = jnp.bfloat16

LN_EPS = 1e-5
A_GROUPS = 8
A_HEAD = 128
A_CHUNK = 128
B_HEADS = 8
B_HEAD = 128
B_CHUNK = 64
B_SUB = 16
XA_HEADS = 4
PEER_HEADS = 8
PEER_NKEYS = 128
PEER_HALF = 128
PEER_TOPK = 16

VMEM_LIMIT_BYTES = 56 * 1024 * 1024


def _cparams(*sem):
    return pltpu.CompilerParams(dimension_semantics=sem, vmem_limit_bytes=VMEM_LIMIT_BYTES)


def _layer_norm_rows(y, g, b):
    mu = jnp.mean(y, axis=-1, keepdims=True)
    yc = y - mu
    var = jnp.mean(yc * yc, axis=-1, keepdims=True)
    return yc * lax.rsqrt(var + LN_EPS) * g + b


def _dot(a, b):
    return jnp.dot(a, b, preferred_element_type=F32)


def _dot_nt(a, b):
    return lax.dot_general(a, b, (((1,), (1,)), ((), ())), preferred_element_type=F32)


def _split3(x):
    p1 = x.astype(BF16)
    r1 = x - p1.astype(F32)
    p2 = r1.astype(BF16)
    r2 = r1 - p2.astype(F32)
    return p1, p2, r2.astype(BF16)


def _matmul_body(x_ref, w_ref, o_ref, xb_ref):
    @pl.when(pl.program_id(1) == 0)
    def _():
        xb_ref[...] = x_ref[...].astype(BF16)

    o_ref[...] = _dot(xb_ref[...], w_ref[...])


def _matmul(x, w, *, tm=512, tn=512):
    m, k = x.shape
    n = w.shape[1]
    return pl.pallas_call(
        _matmul_body,
        grid=(m // tm, n // tn),
        in_specs=[pl.BlockSpec((tm, k), lambda i, j: (i, 0)),
                  pl.BlockSpec((k, tn), lambda i, j: (0, j))],
        out_specs=pl.BlockSpec((tm, tn), lambda i, j: (i, j)),
        out_shape=jax.ShapeDtypeStruct((m, n), F32),
        scratch_shapes=[pltpu.VMEM((tm, k), BF16)],
        compiler_params=_cparams("parallel", "arbitrary"),
        name="proj_matmul",
    )(x, w)


def _sgu_body(u_ref, v_ref, w_ref, bt_ref, g_ref, b_ref, o_ref, *, chunks):
    row = lax.broadcasted_iota(jnp.int32, (A_CHUNK, A_CHUNK), 0)
    col = lax.broadcasted_iota(jnp.int32, (A_CHUNK, A_CHUNK), 1)
    causal = col <= row
    for c in range(chunks):
        rows = slice(c * A_CHUNK, (c + 1) * A_CHUNK)
        for grp in range(A_GROUPS):
            cols = slice(grp * A_HEAD, (grp + 1) * A_HEAD)
            v = jax.nn.gelu(v_ref[rows, cols])
            vn = _layer_norm_rows(v, g_ref[grp:grp + 1, :], b_ref[grp:grp + 1, :])
            w = jnp.where(causal, w_ref[grp], 0.0).astype(BF16)
            z = _dot(w, vn.astype(BF16)) + bt_ref[:, grp:grp + 1]
            o_ref[rows, cols] = jax.nn.gelu(u_ref[rows, cols]) * z


def _sgu(h, sgu_w, sgu_bt, ln_g, ln_b, *, chunks=2):
    t = h.shape[0]
    tt = chunks * A_CHUNK
    width = A_GROUPS * A_HEAD
    full = lambda shape: pl.BlockSpec(shape, lambda i: (0,) * len(shape))
    return pl.pallas_call(
        functools.partial(_sgu_body, chunks=chunks),
        grid=(t // tt,),
        in_specs=[pl.BlockSpec((tt, width), lambda i: (i, 0)),
                  pl.BlockSpec((tt, width), lambda i: (i, 1)),
                  full(sgu_w.shape), full(sgu_bt.shape), full(ln_g.shape), full(ln_b.shape)],
        out_specs=pl.BlockSpec((tt, width), lambda i: (i, 0)),
        out_shape=jax.ShapeDtypeStruct((t, width), F32),
        compiler_params=_cparams("parallel"),
        name="sgu_mixer",
    )(h, h, sgu_w, sgu_bt, ln_g, ln_b)


def _hgrn_body(q_ref, f_ref, i_ref, g_ref, lbl_ref, gn_ref, o_ref, state_ref, *, chunks, layer):
    @pl.when(pl.program_id(2) == 0)
    def _():
        state_ref[...] = jnp.zeros_like(state_ref)

    logits = lbl_ref[...]
    e = jnp.exp(logits - jnp.max(logits, axis=0, keepdims=True))
    lb = jnp.sum(e[:layer + 1], axis=0, keepdims=True) / jnp.sum(e, axis=0, keepdims=True)
    gn = gn_ref[...]

    r = lax.broadcasted_iota(jnp.int32, (B_CHUNK, B_CHUNK), 0)
    c = lax.broadcasted_iota(jnp.int32, (B_CHUNK, B_CHUNK), 1)
    tri = jnp.where(c <= r, 1.0, 0.0).astype(BF16)
    sub_r = lax.broadcasted_iota(jnp.int32, (B_SUB, B_SUB), 0)
    sub_c = lax.broadcasted_iota(jnp.int32, (B_SUB, B_SUB), 1)

    def chunk(ck, carry):
        r0 = pl.multiple_of(ck * B_CHUNK, B_CHUNK)
        rows = pl.ds(r0, B_CHUNK)
        q = q_ref[rows, :]
        iv = i_ref[rows, :]
        f = lb + (1.0 - lb) * (1.0 / (1.0 + jnp.exp(-f_ref[rows, :])))
        k = 1.0 - f
        p1, p2, p3 = _split3(jnp.log(f))
        a = _dot(tri, p1) + _dot(tri, p2) + _dot(tri, p3)
        a_last = a[B_CHUNK - 1:B_CHUNK, :]
        state = state_ref[...]
        ivb = iv.astype(BF16)

        o_inter = _dot_nt((q * jnp.exp(a)).astype(BF16), state.astype(BF16))
        outs = []
        for s in range(B_CHUNK // B_SUB):
            lo = s * B_SUB
            qs = q[lo:lo + B_SUB]
            a_s = a[lo:lo + B_SUB]
            sd = jnp.zeros((B_SUB, B_SUB), F32)
            for j in range(B_SUB):
                dec = jnp.exp(jnp.minimum(a_s - a[lo + j:lo + j + 1, :], 0.0))
                colj = jnp.sum(qs * k[lo + j:lo + j + 1, :] * dec, axis=-1, keepdims=True)
                sd = jnp.where(sub_c == j, colj, sd)
            sd = jnp.where(sub_c <= sub_r, sd, 0.0)
            o_s = o_inter[lo:lo + B_SUB] + _dot(sd.astype(BF16), ivb[lo:lo + B_SUB])
            if s > 0:
                a_ref = a[lo - 1:lo, :]
                qd = qs * jnp.exp(a_s - a_ref)
                kd = k[:lo] * jnp.exp(a_ref - a[:lo])
                so = _dot_nt(qd.astype(BF16), kd.astype(BF16))
                o_s = o_s + _dot(so.astype(BF16), ivb[:lo])
            outs.append(o_s)
        o = jnp.concatenate(outs, axis=0)

        kl = k * jnp.exp(a_last - a)
        state_ref[...] = state * jnp.exp(a_last) + _dot(iv.T.astype(BF16), kl.astype(BF16))

        o = o * lax.rsqrt(jnp.mean(o * o, axis=-1, keepdims=True) + LN_EPS) * gn
        gate = g_ref[rows, :]
        o_ref[rows, :] = o * (gate / (1.0 + jnp.exp(-gate)))
        return carry

    lax.fori_loop(0, chunks, chunk, 0)


def _hgrn(h, lb_logits, gn, *, batch, layer, chunks=8):
    t = h.shape[0]
    seq = t // batch
    tt = chunks * B_CHUNK
    nt = seq // tt
    width = B_HEADS * B_HEAD
    base = 2 * A_GROUPS * A_HEAD // B_HEAD
    sect = lambda s: pl.BlockSpec((tt, B_HEAD), lambda b, hd, i: (b * nt + i, base + s * B_HEADS + hd))
    return pl.pallas_call(
        functools.partial(_hgrn_body, chunks=chunks, layer=layer),
        grid=(batch, B_HEADS, nt),
        in_specs=[sect(0), sect(1), sect(2), sect(3),
                  pl.BlockSpec((lb_logits.shape[0], B_HEAD), lambda b, hd, i: (0, hd)),
                  pl.BlockSpec((1, B_HEAD), lambda b, hd, i: (0, hd))],
        out_specs=pl.BlockSpec((tt, B_HEAD), lambda b, hd, i: (b * nt + i, hd)),
        out_shape=jax.ShapeDtypeStruct((t, width), F32),
        scratch_shapes=[pltpu.VMEM((B_HEAD, B_HEAD), F32)],
        compiler_params=_cparams("parallel", "parallel", "arbitrary"),
        name="hgrn2_mixer",
    )(h, h, h, h, lb_logits, gn)


def _proj_ln_body(*refs, nparts, alpha):
    xs = refs[:nparts]
    ws = refs[nparts:2 * nparts]
    res_ref, g_ref, b_ref, o_ref = refs[2 * nparts:]
    acc = alpha * res_ref[...]
    for x_ref, w_ref in zip(xs, ws):
        acc = acc + _dot(x_ref[...].astype(BF16), w_ref[...])
    o_ref[...] = _layer_norm_rows(acc, g_ref[...], b_ref[...])


def _proj_ln(xs, ws, res, g, b, *, alpha, tm=256):
    t, d = res.shape
    nparts = len(xs)
    in_specs = [pl.BlockSpec((tm, x.shape[1]), lambda i: (i, 0)) for x in xs]
    in_specs += [pl.BlockSpec(w.shape, lambda i: (0, 0)) for w in ws]
    in_specs += [pl.BlockSpec((tm, d), lambda i: (i, 0)),
                 pl.BlockSpec((1, d), lambda i: (0, 0)),
                 pl.BlockSpec((1, d), lambda i: (0, 0))]
    return pl.pallas_call(
        functools.partial(_proj_ln_body, nparts=nparts, alpha=alpha),
        grid=(t // tm,),
        in_specs=in_specs,
        out_specs=pl.BlockSpec((tm, d), lambda i: (i, 0)),
        out_shape=jax.ShapeDtypeStruct((t, d), F32),
        compiler_params=_cparams("parallel"),
        name="proj_residual_ln",
    )(*xs, *ws, res, g, b)


def _xattn_body(q_ref, k_ref, v_ref, o_ref, *, scale):
    hd = q_ref.shape[1] // XA_HEADS
    for hh in range(XA_HEADS):
        cols = slice(hh * hd, (hh + 1) * hd)
        s = _dot_nt(q_ref[:, cols].astype(BF16), k_ref[:, cols].astype(BF16)) * scale
        s = s - jnp.max(s, axis=-1, keepdims=True)
        p = jnp.exp(s)
        p = p / jnp.sum(p, axis=-1, keepdims=True)
        o_ref[:, cols] = _dot(p.astype(BF16), v_ref[:, cols].astype(BF16))


def _xattn(q, k, v, *, batch, tm=512):
    t, d = q.shape
    mlen = k.shape[0] // batch
    nt = t // batch // tm
    return pl.pallas_call(
        functools.partial(_xattn_body, scale=1.0 / math.sqrt(d // XA_HEADS)),
        grid=(batch, nt),
        in_specs=[pl.BlockSpec((tm, d), lambda b, i: (b * nt + i, 0)),
                  pl.BlockSpec((mlen, d), lambda b, i: (b, 0)),
                  pl.BlockSpec((mlen, d), lambda b, i: (b, 0))],
        out_specs=pl.BlockSpec((tm, d), lambda b, i: (b * nt + i, 0)),
        out_shape=jax.ShapeDtypeStruct((t, d), F32),
        compiler_params=_cparams("parallel", "parallel"),
        name="cross_attention",
    )(q, k, v)


def _take_top(cur, n):
    rows = lax.broadcasted_iota(jnp.int32, cur.shape, 0)
    out = []
    for _ in range(n):
        m = jnp.max(cur, axis=0, keepdims=True)
        first = jnp.min(jnp.where(cur == m, rows, cur.shape[0]), axis=0, keepdims=True)
        cur = jnp.where(rows == first, -jnp.inf, cur)
        out.append(m)
    return out


def _peer_route_body(q_ref, k1_ref, k2_ref, s1_ref, s2_ref, e1_ref, e2_ref, tau_ref):
    def scores(keys, qh):
        kh, km, _ = _split3(keys)
        qa, qb, _ = _split3(qh)
        return _dot_nt(kh, qa) + _dot_nt(kh, qb) + _dot_nt(km, qa)

    for hh in range(PEER_HEADS):
        c0 = hh * 2 * PEER_HALF
        s1 = scores(k1_ref[...], q_ref[:, c0:c0 + PEER_HALF])
        s2 = scores(k2_ref[...], q_ref[:, c0 + PEER_HALF:c0 + 2 * PEER_HALF])
        v1 = _take_top(s1, PEER_TOPK)
        v2 = jnp.concatenate(_take_top(s2, PEER_TOPK), axis=0)
        cand = jnp.concatenate([va + v2 for va in v1], axis=0)
        top = _take_top(cand, PEER_TOPK)
        z = jnp.zeros_like(top[0])
        for ts in top:
            z = z + jnp.exp(ts - top[0])
        s1_ref[hh] = s1
        s2_ref[hh] = s2
        e1_ref[hh] = jnp.exp(s1 - v1[0]) / z
        e2_ref[hh] = jnp.exp(s2 - v2[0:1])
        tau_ref[hh] = top[-1]


def _peer_route(qp, k1, k2, *, tp=256):
    t = qp.shape[0]
    big = jax.ShapeDtypeStruct((PEER_HEADS, PEER_NKEYS, t), F32)
    big_spec = pl.BlockSpec((PEER_HEADS, PEER_NKEYS, tp), lambda i: (0, 0, i))
    return pl.pallas_call(
        _peer_route_body,
        grid=(t // tp,),
        in_specs=[pl.BlockSpec((tp, qp.shape[1]), lambda i: (i, 0)),
                  pl.BlockSpec(k1.shape, lambda i: (0, 0)),
                  pl.BlockSpec(k2.shape, lambda i: (0, 0))],
        out_specs=[big_spec, big_spec, big_spec, big_spec,
                   pl.BlockSpec((PEER_HEADS, 1, tp), lambda i: (0, 0, i))],
        out_shape=[big, big, big, big, jax.ShapeDtypeStruct((PEER_HEADS, 1, t), F32)],
        compiler_params=_cparams("parallel"),
        name="peer_route",
    )(qp, k1, k2)


def _peer_expert_body(x_ref, u_ref, vt_ref, s1_ref, s2_ref, e1_ref, e2_ref, tau_ref,
                      g_ref, b_ref, o_ref, xb_ref, acc_ref, *, alpha, et):
    j = pl.program_id(1)

    @pl.when(j == 0)
    def _():
        xb_ref[...] = x_ref[...].astype(BF16)
        acc_ref[...] = jnp.zeros_like(acc_ref)

    ht = _dot_nt(u_ref[...], xb_ref[...])
    ws = []
    for cc in range(et // PEER_NKEYS):
        c = j * (et // PEER_NKEYS) + cc
        gates = jnp.zeros((PEER_NKEYS, x_ref.shape[0]), F32)
        for hh in range(PEER_HEADS):
            total = s1_ref[hh, pl.ds(c, 1), :] + s2_ref[hh]
            gates = gates + jnp.where(total >= tau_ref[hh], e1_ref[hh, pl.ds(c, 1), :] * e2_ref[hh], 0.0)
        hcc = ht[cc * PEER_NKEYS:(cc + 1) * PEER_NKEYS]
        ws.append((jax.nn.gelu(hcc) * gates).astype(BF16))
    w = jnp.concatenate(ws, axis=0)
    acc_ref[...] += _dot(vt_ref[...], w)

    @pl.when(j == pl.num_programs(1) - 1)
    def _():
        y = alpha * x_ref[...] + acc_ref[...].T
        o_ref[...] = _layer_norm_rows(y, g_ref[...], b_ref[...])


def _peer_experts(x, u, vt, s1, s2, e1, e2, tau, g, b, *, alpha, tt=512, et=256):
    t, d = x.shape
    ne = u.shape[0]
    stat = pl.BlockSpec((PEER_HEADS, PEER_NKEYS, tt), lambda i, j: (0, 0, i))
    return pl.pallas_call(
        functools.partial(_peer_expert_body, alpha=alpha, et=et),
        grid=(t // tt, ne // et),
        in_specs=[pl.BlockSpec((tt, d), lambda i, j: (i, 0)),
                  pl.BlockSpec((et, d), lambda i, j: (j, 0)),
                  pl.BlockSpec((d, et), lambda i, j: (0, j)),
                  stat, stat, stat, stat,
                  pl.BlockSpec((PEER_HEADS, 1, tt), lambda i, j: (0, 0, i)),
                  pl.BlockSpec((1, d), lambda i, j: (0, 0)),
                  pl.BlockSpec((1, d), lambda i, j: (0, 0))],
        out_specs=pl.BlockSpec((tt, d), lambda i, j: (i, 0)),
        out_shape=jax.ShapeDtypeStruct((t, d), F32),
        scratch_shapes=[pltpu.VMEM((tt, d), BF16), pltpu.VMEM((d, tt), F32)],
        compiler_params=_cparams("parallel", "arbitrary"),
        name="peer_experts",
    )(x, u, vt, s1, s2, e1, e2, tau, g, b)


def kernel(x, mem, w_in, sgu_w, sgu_b, sgu_ln_g, sgu_ln_b, hgrn_lb_logits, hgrn_norm_g, w_out, ln1_g, ln1_b, xa_wq, xa_wk, xa_wv, xa_wo, ln2_g, ln2_b, peer_wq, peer_k1, peer_k2, peer_u, peer_v, ln3_g, ln3_b):
    batch, seq, d = x.shape
    depth = w_in.shape[0]
    alpha = (2.0 * depth) ** 0.25
    a_width = A_GROUPS * A_HEAD
    xt = x.reshape(batch * seq, d)
    memt = mem.reshape(batch * mem.shape[1], d)
    row = lambda p: p.reshape(1, -1)
    for l in range(depth):
        h = _matmul(xt, w_in[l].astype(BF16))
        ya = _sgu(h, sgu_w[l], sgu_b[l].T, sgu_ln_g[l], sgu_ln_b[l])
        yb = _hgrn(h, hgrn_lb_logits, row(hgrn_norm_g[l]), batch=batch, layer=l)
        wo = w_out[l].astype(BF16)
        xt = _proj_ln([ya, yb], [wo[:a_width], wo[a_width:]], xt, row(ln1_g[l]), row(ln1_b[l]), alpha=alpha)
        q = _matmul(xt, xa_wq[l].astype(BF16))
        k = _matmul(memt, xa_wk[l].astype(BF16))
        v = _matmul(memt, xa_wv[l].astype(BF16))
        att = _xattn(q, k, v, batch=batch)
        xt = _proj_ln([att], [xa_wo[l].astype(BF16)], xt, row(ln2_g[l]), row(ln2_b[l]), alpha=alpha)
        qp = _matmul(xt, peer_wq[l].astype(BF16))
        s1, s2, e1, e2, tau = _peer_route(qp, peer_k1[l], peer_k2[l])
        xt = _peer_experts(xt, peer_u[l].astype(BF16), peer_v[l].T.astype(BF16), s1, s2, e1, e2, tau,
                           row(ln3_g[l]), row(ln3_b[l]), alpha=alpha)
    return xt.reshape(batch, seq, d)
```

```python
import functools
import math

import jax
import jax.numpy as jnp
from jax import lax
from jax.experimental import pallas as pl
from jax.experimental.pallas import tpu as pltpu

F32 = jnp.float32
BF16 = jnp.bfloat16

LN_EPS = 1e-5
A_GROUPS = 8
A_HEAD = 128
A_CHUNK = 128
B_HEADS = 8
B_HEAD = 128
B_CHUNK = 64
B_SUB = 16
XA_HEADS = 4
PEER_HEADS = 8
PEER_NKEYS = 128
PEER_HALF = 128
PEER_TOPK = 16
SUBLANES = 8
LANES = 128

VMEM_LIMIT_BYTES = 56 * 1024 * 1024


def _cparams(*sem):
    return pltpu.CompilerParams(dimension_semantics=sem, vmem_limit_bytes=VMEM_LIMIT_BYTES)


def _layer_norm_rows(y, g, b):
    mu = jnp.mean(y, axis=-1, keepdims=True)
    yc = y - mu
    var = jnp.mean(yc * yc, axis=-1, keepdims=True)
    return yc * lax.rsqrt(var + LN_EPS) * g + b


def _dot(a, b):
    return jnp.dot(a, b, preferred_element_type=F32)


def _dot_nt(a, b):
    return lax.dot_general(a, b, (((1,), (1,)), ((), ())), preferred_element_type=F32)


def _split3(x):
    p1 = x.astype(BF16)
    r1 = x - p1.astype(F32)
    p2 = r1.astype(BF16)
    r2 = r1 - p2.astype(F32)
    return p1, p2, r2.astype(BF16)


def _matmul_body(x_ref, w_ref, o_ref, xb_ref):
    @pl.when(pl.program_id(1) == 0)
    def _():
        xb_ref[...] = x_ref[...].astype(BF16)

    o_ref[...] = _dot(xb_ref[...], w_ref[...])


def _matmul(x, w, *, tm=512, tn=512):
    m, k = x.shape
    n = w.shape[1]
    return pl.pallas_call(
        _matmul_body,
        grid=(m // tm, n // tn),
        in_specs=[pl.BlockSpec((tm, k), lambda i, j: (i, 0)),
                  pl.BlockSpec((k, tn), lambda i, j: (0, j))],
        out_specs=pl.BlockSpec((tm, tn), lambda i, j: (i, j)),
        out_shape=jax.ShapeDtypeStruct((m, n), F32),
        scratch_shapes=[pltpu.VMEM((tm, k), BF16)],
        compiler_params=_cparams("parallel", "arbitrary"),
        name="proj_matmul",
    )(x, w)


def _sgu_body(u_ref, v_ref, w_ref, bt_ref, g_ref, b_ref, o_ref, *, chunks):
    row = lax.broadcasted_iota(jnp.int32, (A_CHUNK, A_CHUNK), 0)
    col = lax.broadcasted_iota(jnp.int32, (A_CHUNK, A_CHUNK), 1)
    causal = col <= row
    for c in range(chunks):
        rows = slice(c * A_CHUNK, (c + 1) * A_CHUNK)
        for grp in range(A_GROUPS):
            cols = slice(grp * A_HEAD, (grp + 1) * A_HEAD)
            v = jax.nn.gelu(v_ref[rows, cols])
            vn = _layer_norm_rows(v, g_ref[grp:grp + 1, :], b_ref[grp:grp + 1, :])
            w = jnp.where(causal, w_ref[grp], 0.0).astype(BF16)
            z = _dot(w, vn.astype(BF16)) + bt_ref[:, grp:grp + 1]
            o_ref[rows, cols] = jax.nn.gelu(u_ref[rows, cols]) * z


def _sgu(h, sgu_w, sgu_bt, ln_g, ln_b, *, chunks=2):
    t = h.shape[0]
    tt = chunks * A_CHUNK
    width = A_GROUPS * A_HEAD
    full = lambda shape: pl.BlockSpec(shape, lambda i: (0,) * len(shape))
    return pl.pallas_call(
        functools.partial(_sgu_body, chunks=chunks),
        grid=(t // tt,),
        in_specs=[pl.BlockSpec((tt, width), lambda i: (i, 0)),
                  pl.BlockSpec((tt, width), lambda i: (i, 1)),
                  full(sgu_w.shape), full(sgu_bt.shape), full(ln_g.shape), full(ln_b.shape)],
        out_specs=pl.BlockSpec((tt, width), lambda i: (i, 0)),
        out_shape=jax.ShapeDtypeStruct((t, width), F32),
        compiler_params=_cparams("parallel"),
        name="sgu_mixer",
    )(h, h, sgu_w, sgu_bt, ln_g, ln_b)


def _hgrn_chunk_head(q, fl, iv, gate, lb, gn, state, tri, sub_r, sub_c):
    f = lb + (1.0 - lb) * (1.0 / (1.0 + jnp.exp(-fl)))
    k = 1.0 - f
    p1, p2, p3 = _split3(jnp.log(f))
    a = _dot(tri, p1) + _dot(tri, p2) + _dot(tri, p3)
    a_last = a[B_CHUNK - 1:B_CHUNK, :]
    ivb = iv.astype(BF16)

    o_inter = _dot_nt((q * jnp.exp(a)).astype(BF16), state.astype(BF16))
    outs = []
    for s in range(B_CHUNK // B_SUB):
        lo = s * B_SUB
        qs = q[lo:lo + B_SUB]
        a_s = a[lo:lo + B_SUB]
        sd = jnp.zeros((B_SUB, B_SUB), F32)
        for j in range(B_SUB):
            dec = jnp.exp(jnp.minimum(a_s - a[lo + j:lo + j + 1, :], 0.0))
            colj = jnp.sum(qs * k[lo + j:lo + j + 1, :] * dec, axis=-1, keepdims=True)
            sd = jnp.where(sub_c == j, colj, sd)
        sd = jnp.where(sub_c <= sub_r, sd, 0.0)
        o_s = o_inter[lo:lo + B_SUB] + _dot(sd.astype(BF16), ivb[lo:lo + B_SUB])
        if s > 0:
            a_ref = a[lo - 1:lo, :]
            qd = qs * jnp.exp(a_s - a_ref)
            kd = k[:lo] * jnp.exp(a_ref - a[:lo])
            so = _dot_nt(qd.astype(BF16), kd.astype(BF16))
            o_s = o_s + _dot(so.astype(BF16), ivb[:lo])
        outs.append(o_s)
    o = jnp.concatenate(outs, axis=0)

    kl = k * jnp.exp(a_last - a)
    new_state = state * jnp.exp(a_last) + _dot(iv.T.astype(BF16), kl.astype(BF16))
    o = o * lax.rsqrt(jnp.mean(o * o, axis=-1, keepdims=True) + LN_EPS) * gn
    return o * (gate / (1.0 + jnp.exp(-gate))), new_state


def _hgrn_body(q_ref, f_ref, i_ref, g_ref, lbl_ref, gn_ref, o_ref, state_ref, *, chunks, layer, heads):
    @pl.when(pl.program_id(2) == 0)
    def _():
        state_ref[...] = jnp.zeros_like(state_ref)

    logits = lbl_ref[...]
    e = jnp.exp(logits - jnp.max(logits, axis=0, keepdims=True))
    lb = jnp.sum(e[:layer + 1], axis=0, keepdims=True) / jnp.sum(e, axis=0, keepdims=True)
    gn = gn_ref[...]

    r = lax.broadcasted_iota(jnp.int32, (B_CHUNK, B_CHUNK), 0)
    c = lax.broadcasted_iota(jnp.int32, (B_CHUNK, B_CHUNK), 1)
    tri = jnp.where(c <= r, 1.0, 0.0).astype(BF16)
    sub_r = lax.broadcasted_iota(jnp.int32, (B_SUB, B_SUB), 0)
    sub_c = lax.broadcasted_iota(jnp.int32, (B_SUB, B_SUB), 1)

    def chunk(ck, carry):
        rows = pl.ds(pl.multiple_of(ck * B_CHUNK, B_CHUNK), B_CHUNK)
        for hd in range(heads):
            cols = slice(hd * B_HEAD, (hd + 1) * B_HEAD)
            o, new_state = _hgrn_chunk_head(
                q_ref[rows, cols], f_ref[rows, cols], i_ref[rows, cols], g_ref[rows, cols],
                lb[:, cols], gn[:, cols], state_ref[hd], tri, sub_r, sub_c)
            state_ref[hd] = new_state
            o_ref[rows, cols] = o
        return carry

    lax.fori_loop(0, chunks, chunk, 0)


def _hgrn(h, lb_logits, gn, *, batch, layer, chunks=8, heads=4):
    t = h.shape[0]
    seq = t // batch
    tt = chunks * B_CHUNK
    nt = seq // tt
    width = B_HEADS * B_HEAD
    gw = heads * B_HEAD
    base = 2 * A_GROUPS * A_HEAD // gw
    sect = lambda s: pl.BlockSpec((tt, gw), lambda b, hg, i: (b * nt + i, base + s * (width // gw) + hg))
    return pl.pallas_call(
        functools.partial(_hgrn_body, chunks=chunks, layer=layer, heads=heads),
        grid=(batch, B_HEADS // heads, nt),
        in_specs=[sect(0), sect(1), sect(2), sect(3),
                  pl.BlockSpec((lb_logits.shape[0], gw), lambda b, hg, i: (0, hg)),
                  pl.BlockSpec((1, gw), lambda b, hg, i: (0, hg))],
        out_specs=pl.BlockSpec((tt, gw), lambda b, hg, i: (b * nt + i, hg)),
        out_shape=jax.ShapeDtypeStruct((t, width), F32),
        scratch_shapes=[pltpu.VMEM((heads, B_HEAD, B_HEAD), F32)],
        compiler_params=_cparams("parallel", "parallel", "arbitrary"),
        name="hgrn2_mixer",
    )(h, h, h, h, lb_logits, gn)


def _proj_ln_body(*refs, nparts, alpha):
    xs = refs[:nparts]
    ws = refs[nparts:2 * nparts]
    res_ref, g_ref, b_ref, o_ref = refs[2 * nparts:]
    acc = alpha * res_ref[...]
    for x_ref, w_ref in zip(xs, ws):
        acc = acc + _dot(x_ref[...].astype(BF16), w_ref[...])
    o_ref[...] = _layer_norm_rows(acc, g_ref[...], b_ref[...])


def _proj_ln(xs, ws, res, g, b, *, alpha, tm=256):
    t, d = res.shape
    nparts = len(xs)
    in_specs = [pl.BlockSpec((tm, x.shape[1]), lambda i: (i, 0)) for x in xs]
    in_specs += [pl.BlockSpec(w.shape, lambda i: (0, 0)) for w in ws]
    in_specs += [pl.BlockSpec((tm, d), lambda i: (i, 0)),
                 pl.BlockSpec((1, d), lambda i: (0, 0)),
                 pl.BlockSpec((1, d), lambda i: (0, 0))]
    return pl.pallas_call(
        functools.partial(_proj_ln_body, nparts=nparts, alpha=alpha),
        grid=(t // tm,),
        in_specs=in_specs,
        out_specs=pl.BlockSpec((tm, d), lambda i: (i, 0)),
        out_shape=jax.ShapeDtypeStruct((t, d), F32),
        compiler_params=_cparams("parallel"),
        name="proj_residual_ln",
    )(*xs, *ws, res, g, b)


def _xattn_body(q_ref, k_ref, v_ref, o_ref, *, scale):
    hd = q_ref.shape[1] // XA_HEADS
    for hh in range(XA_HEADS):
        cols = slice(hh * hd, (hh + 1) * hd)
        s = _dot_nt(q_ref[:, cols].astype(BF16), k_ref[:, cols].astype(BF16)) * scale
        s = s - jnp.max(s, axis=-1, keepdims=True)
        p = jnp.exp(s)
        p = p / jnp.sum(p, axis=-1, keepdims=True)
        o_ref[:, cols] = _dot(p.astype(BF16), v_ref[:, cols].astype(BF16))


def _xattn(q, k, v, *, batch, tm=512):
    t, d = q.shape
    mlen = k.shape[0] // batch
    nt = t // batch // tm
    return pl.pallas_call(
        functools.partial(_xattn_body, scale=1.0 / math.sqrt(d // XA_HEADS)),
        grid=(batch, nt),
        in_specs=[pl.BlockSpec((tm, d), lambda b, i: (b * nt + i, 0)),
                  pl.BlockSpec((mlen, d), lambda b, i: (b, 0)),
                  pl.BlockSpec((mlen, d), lambda b, i: (b, 0))],
        out_specs=pl.BlockSpec((tm, d), lambda b, i: (b * nt + i, 0)),
        out_shape=jax.ShapeDtypeStruct((t, d), F32),
        compiler_params=_cparams("parallel", "parallel"),
        name="cross_attention",
    )(q, k, v)


def _sort_network(n):
    pairs = []
    p = 1
    while p < n:
        k = p
        while k >= 1:
            for j in range(k % p, n - k, 2 * k):
                for i in range(min(k, n - j - k)):
                    if (i + j) // (2 * p) == (i + j + k) // (2 * p):
                        pairs.append((i + j, i + j + k))
            k //= 2
        p *= 2
    return pairs


def _compare_exchange(blocks, i, j):
    hi, lo = blocks[i], blocks[j]
    if lo is None:
        return
    if hi is None:
        blocks[i], blocks[j] = lo, None
        return
    blocks[i], blocks[j] = jnp.maximum(hi, lo), jnp.minimum(hi, lo)


def _top_values(blocks):
    blocks = list(blocks) + [None] * (PEER_TOPK - len(blocks))
    for i, j in _sort_network(PEER_TOPK):
        _compare_exchange(blocks, i, j)
    shift = SUBLANES // 2
    while shift >= 1:
        partner = [None if b is None else pltpu.roll(b, shift, 0) for b in blocks]
        merged = []
        for d in range(PEER_TOPK):
            a, b = blocks[d], partner[PEER_TOPK - 1 - d]
            merged.append(b if a is None else a if b is None else jnp.maximum(a, b))
        blocks = merged
        stride = PEER_TOPK // 2
        while stride >= 1:
            for i in range(PEER_TOPK):
                if i & stride == 0:
                    _compare_exchange(blocks, i, i + stride)
            stride //= 2
        shift //= 2
    return blocks


def _rows_from_ranks(blocks, first):
    sub = lax.broadcasted_iota(jnp.int32, blocks[first].shape, 0)
    out = blocks[first]
    for r in range(1, SUBLANES):
        out = jnp.where(sub == r, blocks[first + r], out)
    return out


def _peer_route_body(q_ref, k1_ref, k2_ref, s1_ref, s2_ref, e1_ref, e2_ref, tau_ref):
    def scores(keys, qh):
        kh, km, _ = _split3(keys)
        qa, qb, _ = _split3(qh)
        return _dot_nt(kh, qa) + _dot_nt(kh, qb) + _dot_nt(km, qa)

    def blocks_of(s):
        return [s[r:r + SUBLANES] for r in range(0, s.shape[0], SUBLANES)]

    for hh in range(PEER_HEADS):
        c0 = hh * 2 * PEER_HALF
        s1 = scores(k1_ref[...], q_ref[:, c0:c0 + PEER_HALF])
        s2 = scores(k2_ref[...], q_ref[:, c0 + PEER_HALF:c0 + 2 * PEER_HALF])
        v1 = _top_values(blocks_of(s1))
        v2 = _top_values(blocks_of(s2))
        v2_lo, v2_hi = _rows_from_ranks(v2, 0), _rows_from_ranks(v2, SUBLANES)
        cand = [v1[0] + v2_lo, v1[0] + v2_hi]
        cand += [v1[a] + v2_lo for a in range(1, SUBLANES)]
        cand += [_rows_from_ranks(v1, SUBLANES) + v2[0]]
        top = _top_values(cand)
        z = jnp.zeros_like(top[0])
        for ts in top:
            z = z + jnp.exp(ts - top[0])
        s1_ref[hh] = s1
        s2_ref[hh] = s2
        e1_ref[hh] = jnp.exp(s1 - v1[0][0:1]) / z[0:1]
        e2_ref[hh] = jnp.exp(s2 - v2[0][0:1])
        tau_ref[hh] = top[PEER_TOPK - 1][0:1]


def _peer_route(qp, k1, k2, *, tp=256):
    t = qp.shape[0]
    big = jax.ShapeDtypeStruct((PEER_HEADS, PEER_NKEYS, t), F32)
    big_spec = pl.BlockSpec((PEER_HEADS, PEER_NKEYS, tp), lambda i: (0, 0, i))
    return pl.pallas_call(
        _peer_route_body,
        grid=(t // tp,),
        in_specs=[pl.BlockSpec((tp, qp.shape[1]), lambda i: (i, 0)),
                  pl.BlockSpec(k1.shape, lambda i: (0, 0)),
                  pl.BlockSpec(k2.shape, lambda i: (0, 0))],
        out_specs=[big_spec, big_spec, big_spec, big_spec,
                   pl.BlockSpec((PEER_HEADS, 1, tp), lambda i: (0, 0, i))],
        out_shape=[big, big, big, big, jax.ShapeDtypeStruct((PEER_HEADS, 1, t), F32)],
        compiler_params=_cparams("parallel"),
        name="peer_route",
    )(qp, k1, k2)


def _peer_expert_body(x_ref, u_ref, vt_ref, s1_ref, s2_ref, e1_ref, e2_ref, tau_ref,
                      g_ref, b_ref, o_ref, xb_ref, acc_ref, ht_ref, w_ref, *, alpha, et):
    j = pl.program_id(1)
    tt = x_ref.shape[0]

    @pl.when(j == 0)
    def _():
        xb_ref[...] = x_ref[...].astype(BF16)
        acc_ref[...] = jnp.zeros_like(acc_ref)

    ht_ref[...] = _dot_nt(u_ref[...], xb_ref[...])
    for cc in range(et // PEER_NKEYS):
        c = j * (et // PEER_NKEYS) + cc
        rows = slice(cc * PEER_NKEYS, (cc + 1) * PEER_NKEYS)
        s1_rows = [s1_ref[hh, pl.ds(c, 1), :] for hh in range(PEER_HEADS)]
        e1_rows = [e1_ref[hh, pl.ds(c, 1), :] for hh in range(PEER_HEADS)]
        for lb in range(tt // LANES):
            lanes = slice(lb * LANES, (lb + 1) * LANES)
            gates = jnp.zeros((PEER_NKEYS, LANES), F32)
            for hh in range(PEER_HEADS):
                total = s1_rows[hh][:, lanes] + s2_ref[hh, :, lanes]
                gate = e1_rows[hh][:, lanes] * e2_ref[hh, :, lanes]
                gates = gates + jnp.where(total >= tau_ref[hh, :, lanes], gate, 0.0)
            w_ref[rows, lanes] = (jax.nn.gelu(ht_ref[rows, lanes]) * gates).astype(BF16)
    acc_ref[...] += _dot(vt_ref[...], w_ref[...])

    @pl.when(j == pl.num_programs(1) - 1)
    def _():
        y = alpha * x_ref[...] + acc_ref[...].T
        o_ref[...] = _layer_norm_rows(y, g_ref[...], b_ref[...])


def _peer_experts(x, u, vt, s1, s2, e1, e2, tau, g, b, *, alpha, tt=512, et=512):
    t, d = x.shape
    ne = u.shape[0]
    stat = pl.BlockSpec((PEER_HEADS, PEER_NKEYS, tt), lambda i, j: (0, 0, i))
    return pl.pallas_call(
        functools.partial(_peer_expert_body, alpha=alpha, et=et),
        grid=(t // tt, ne // et),
        in_specs=[pl.BlockSpec((tt, d), lambda i, j: (i, 0), pipeline_mode=pl.Buffered(1)),
                  pl.BlockSpec((et, d), lambda i, j: (j, 0)),
                  pl.BlockSpec((d, et), lambda i, j: (0, j)),
                  stat, stat, stat, stat,
                  pl.BlockSpec((PEER_HEADS, 1, tt), lambda i, j: (0, 0, i)),
                  pl.BlockSpec((1, d), lambda i, j: (0, 0)),
                  pl.BlockSpec((1, d), lambda i, j: (0, 0))],
        out_specs=pl.BlockSpec((tt, d), lambda i, j: (i, 0)),
        out_shape=jax.ShapeDtypeStruct((t, d), F32),
        scratch_shapes=[pltpu.VMEM((tt, d), BF16), pltpu.VMEM((d, tt), F32),
                        pltpu.VMEM((et, tt), F32), pltpu.VMEM((et, tt), BF16)],
        compiler_params=_cparams("parallel", "arbitrary"),
        name="peer_experts",
    )(x, u, vt, s1, s2, e1, e2, tau, g, b)


def kernel(x, mem, w_in, sgu_w, sgu_b, sgu_ln_g, sgu_ln_b, hgrn_lb_logits, hgrn_norm_g, w_out, ln1_g, ln1_b, xa_wq, xa_wk, xa_wv, xa_wo, ln2_g, ln2_b, peer_wq, peer_k1, peer_k2, peer_u, peer_v, ln3_g, ln3_b):
    batch, seq, d = x.shape
    depth = w_in.shape[0]
    alpha = (2.0 * depth) ** 0.25
    a_width = A_GROUPS * A_HEAD
    xt = x.reshape(batch * seq, d)
    memt = mem.reshape(batch * mem.shape[1], d)
    row = lambda p: p.reshape(1, -1)
    for l in range(depth):
        h = _matmul(xt, w_in[l].astype(BF16))
        ya = _sgu(h, sgu_w[l], sgu_b[l].T, sgu_ln_g[l], sgu_ln_b[l])
        yb = _hgrn(h, hgrn_lb_logits, row(hgrn_norm_g[l]), batch=batch, layer=l)
        wo = w_out[l].astype(BF16)
        xt = _proj_ln([ya, yb], [wo[:a_width], wo[a_width:]], xt, row(ln1_g[l]), row(ln1_b[l]), alpha=alpha)
        q = _matmul(xt, xa_wq[l].astype(BF16))
        k = _matmul(memt, xa_wk[l].astype(BF16))
        v = _matmul(memt, xa_wv[l].astype(BF16))
        att = _xattn(q, k, v, batch=batch)
        xt = _proj_ln([att], [xa_wo[l].astype(BF16)], xt, row(ln2_g[l]), row(ln2_b[l]), alpha=alpha)
        qp = _matmul(xt, peer_wq[l].astype(BF16))
        s1, s2, e1, e2, tau = _peer_route(qp, peer_k1[l], peer_k2[l])
        xt = _peer_experts(xt, peer_u[l].astype(BF16), peer_v[l].T.astype(BF16), s1, s2, e1, e2, tau,
                           row(ln3_g[l]), row(ln3_b[l]), alpha=alpha)
    return xt.reshape(batch, seq, d)
```
